```python
import jax, jax.numpy as jnp
from jax import lax
import numpy as np

D_MODEL = 1024
BATCH = 8
SEQ = 2048
DEPTH = 1

ATT_HEADS = 8
ATT_KV_HEADS = 2
ATT_GROUP = ATT_HEADS // ATT_KV_HEADS
ATT_HEAD_DIM = 64
WINDOW = 128
ATT_BLOCK = WINDOW
ROPE_THETA = 500000.0
ROPE_DIMS = ATT_HEAD_DIM // 4

RET_HEADS = 8
RET_QK_DIM = 64
RET_V_DIM = 128
RET_CHUNK = 128
RET_THETA = 10000.0

ATT_Q = ATT_HEADS * ATT_HEAD_DIM
ATT_KV = ATT_KV_HEADS * ATT_HEAD_DIM
RET_QK = RET_HEADS * RET_QK_DIM
RET_V = RET_HEADS * RET_V_DIM
IN_SIZES = (ATT_Q, ATT_KV, ATT_KV, RET_QK, RET_QK, RET_V, RET_V, D_MODEL, D_MODEL)
IN_WIDTH = 5888

PEER_HEADS = 8
N_KEYS = 128
N_EXPERTS = N_KEYS * N_KEYS
PEER_QUERY_DIM = 256
PEER_HALF = PEER_QUERY_DIM // 2
PEER_TOPK = 16
PEER_TOKEN_BLOCK = 128

EPS = 1e-6
F32 = jnp.float32

kernel_name = 'hybrid_swa_sink_retention_peer_block'


def rms_norm(x, gain):
    xf = x.astype(F32)
    y = xf * lax.rsqrt(jnp.mean(xf * xf, axis=-1, keepdims=True) + EPS)
    return (y * gain.astype(F32)).astype(x.dtype)


def split_columns(t, sizes):
    outs, start = [], 0
    for s in sizes:
        outs.append(t[..., start:start + s])
        start += s
    return outs


def apply_rotary(x, positions, rot_dims, theta):
    half = rot_dims // 2
    inv_freq = theta ** (-jnp.arange(half, dtype=F32) / half)
    ang = positions.astype(F32)[:, None] * inv_freq[None, :]
    cos = jnp.cos(ang)[:, None, :]
    sin = jnp.sin(ang)[:, None, :]
    xf = x.astype(F32)
    x1 = xf[..., :half]
    x2 = xf[..., half:rot_dims]
    out = jnp.concatenate([x1 * cos - x2 * sin, x2 * cos + x1 * sin, xf[..., rot_dims:]], axis=-1)
    return out.astype(x.dtype)


def sliding_window_attention(q, k, v, sinks):
    B, S = q.shape[0], q.shape[1]
    nb = S // ATT_BLOCK
    qb = q.reshape(B, nb, ATT_BLOCK, ATT_KV_HEADS, ATT_GROUP, ATT_HEAD_DIM)

    def with_previous_block(t):
        tb = t.reshape(B, nb, ATT_BLOCK, ATT_KV_HEADS, ATT_HEAD_DIM)
        prev = jnp.pad(tb[:, :-1], ((0, 0), (1, 0), (0, 0), (0, 0), (0, 0)))
        return jnp.concatenate([prev, tb], axis=2)

    kb = with_previous_block(k)
    vb = with_previous_block(v)
    s = jnp.einsum('bnqhgd,bnkhd->bnhgqk', qb, kb, preferred_element_type=F32)
    s = s * (ATT_HEAD_DIM ** -0.5)
    qi = jnp.arange(ATT_BLOCK)[:, None] + ATT_BLOCK
    kj = jnp.arange(2 * ATT_BLOCK)[None, :]
    diff = qi - kj
    band = (diff >= 0) & (diff < WINDOW)
    key_pos = jnp.arange(nb)[:, None] * ATT_BLOCK + kj - ATT_BLOCK
    mask = band[None] & (key_pos >= 0)[:, None, :]
    s = jnp.where(mask[None, :, None, None], s, -jnp.inf)
    sink = sinks.astype(F32).reshape(ATT_KV_HEADS, ATT_GROUP)[None, None, :, :, None, None]
    m = jnp.maximum(jnp.max(s, axis=-1, keepdims=True), sink)
    p = jnp.exp(s - m)
    p = p / (jnp.sum(p, axis=-1, keepdims=True) + jnp.exp(sink - m))
    o = jnp.einsum('bnhgqk,bnkhd->bnqhgd', p.astype(v.dtype), vb)
    return o.reshape(B, S, ATT_Q)


def chunkwise_retention(q, k, v):
    B, S = q.shape[0], q.shape[1]
    C = RET_CHUNK
    nc = S // C
    qc = q.astype(F32).reshape(B, nc, C, RET_HEADS, RET_QK_DIM)
    kc = k.astype(F32).reshape(B, nc, C, RET_HEADS, RET_QK_DIM)
    vc = v.astype(F32).reshape(B, nc, C, RET_HEADS, RET_V_DIM)
    log_gamma = jnp.log1p(-jnp.exp2(-5.0 - jnp.arange(RET_HEADS, dtype=F32)))
    idx = jnp.arange(C, dtype=F32)
    diff = idx[:, None] - idx[None, :]
    decay_inner = jnp.where(diff[None] >= 0,
                            jnp.exp(log_gamma[:, None, None] * jnp.maximum(diff, 0.0)[None]), 0.0)
    scores = jnp.einsum('bnihd,bnjhd->bnhij', qc, kc) * decay_inner[None, None]
    o_inner = jnp.einsum('bnhij,bnjhe->bnihe', scores, vc)
    w_key = jnp.exp(log_gamma[:, None] * (C - 1.0 - idx)[None, :])
    kv = jnp.einsum('bnjhd,hj,bnjhe->bnhde', kc, w_key, vc)
    chunk_decay = jnp.exp(log_gamma * C)[None, :, None, None]

    def step(state, kv_n):
        return state * chunk_decay + kv_n, state

    init = jnp.zeros((B, RET_HEADS, RET_QK_DIM, RET_V_DIM), F32)
    _, states = lax.scan(step, init, jnp.moveaxis(kv, 1, 0))
    states = jnp.moveaxis(states, 0, 1)
    w_query = jnp.exp(log_gamma[:, None] * (idx + 1.0)[None, :])
    o_cross = jnp.einsum('bnihd,hi,bnhde->bnihe', qc, w_query, states)
    return (o_inner + o_cross).reshape(B, S, RET_HEADS, RET_V_DIM)


def hybrid_mixer(hn, positions, w_in, att_sinks, w_att_branch, w_ret_branch, w_out):
    B, S = hn.shape[0], hn.shape[1]
    proj = hn @ w_in
    q_a, k_a, v_a, q_r, k_r, v_r, g_r, gate_a, gate_r = split_columns(proj, IN_SIZES)
    q_a = apply_rotary(q_a.reshape(B, S, ATT_HEADS, ATT_HEAD_DIM), positions, ROPE_DIMS, ROPE_THETA)
    k_a = apply_rotary(k_a.reshape(B, S, ATT_KV_HEADS, ATT_HEAD_DIM), positions, ROPE_DIMS, ROPE_THETA)
    v_a = v_a.reshape(B, S, ATT_KV_HEADS, ATT_HEAD_DIM)
    att = sliding_window_attention(q_a, k_a, v_a, att_sinks)
    q_r = apply_rotary(q_r.reshape(B, S, RET_HEADS, RET_QK_DIM), positions, RET_QK_DIM, RET_THETA)
    k_r = apply_rotary(k_r.reshape(B, S, RET_HEADS, RET_QK_DIM), positions, RET_QK_DIM, RET_THETA) * (RET_QK_DIM ** -0.5)
    ret = chunkwise_retention(q_r, k_r, v_r.reshape(B, S, RET_HEADS, RET_V_DIM))
    ret = ret * lax.rsqrt(jnp.mean(ret * ret, axis=-1, keepdims=True) + EPS)
    ret = jax.nn.silu(g_r) * ret.reshape(B, S, RET_V).astype(hn.dtype)
    merged = jax.nn.sigmoid(gate_a) * (att @ w_att_branch) + jax.nn.sigmoid(gate_r) * (ret @ w_ret_branch)
    return merged @ w_out


def peer_ffn(hn, w_query, sub_keys_1, sub_keys_2, expert_u, expert_v):
    B, S, D = hn.shape
    T = B * S
    hf = hn.reshape(T, D)
    q = (hf @ w_query).reshape(T, PEER_HEADS, 2, PEER_HALF)
    s1 = jnp.einsum('thc,kc->thk', q[:, :, 0], sub_keys_1, preferred_element_type=F32)
    s2 = jnp.einsum('thc,kc->thk', q[:, :, 1], sub_keys_2, preferred_element_type=F32)
    v1, i1 = lax.top_k(s1, PEER_TOPK)
    v2, i2 = lax.top_k(s2, PEER_TOPK)
    cand_score = (v1[..., :, None] + v2[..., None, :]).reshape(T, PEER_HEADS, PEER_TOPK * PEER_TOPK)
    cand_index = (i1[..., :, None] * N_KEYS + i2[..., None, :]).reshape(T, PEER_HEADS, PEER_TOPK * PEER_TOPK)
    top_score, pick = lax.top_k(cand_score, PEER_TOPK)
    expert_index = jnp.take_along_axis(cand_index, pick, axis=-1)
    gate = jax.nn.softmax(top_score, axis=-1)
    nblk = T // PEER_TOKEN_BLOCK

    def block_fn(args):
        xb, ib, gb = args
        u = expert_u[ib]
        a = jnp.einsum('td,thkd->thk', xb, u, preferred_element_type=F32)
        coef = gb * jax.nn.gelu(a, approximate=False)
        vv = expert_v[ib]
        return jnp.einsum('thk,thkd->td', coef.astype(vv.dtype), vv)

    out = lax.map(block_fn, (hf.reshape(nblk, PEER_TOKEN_BLOCK, D),
                             expert_index.reshape(nblk, PEER_TOKEN_BLOCK, PEER_HEADS, PEER_TOPK),
                             gate.reshape(nblk, PEER_TOKEN_BLOCK, PEER_HEADS, PEER_TOPK)))
    return out.reshape(B, S, D).astype(hn.dtype)


def setup_inputs(seed: int = 0) -> dict:
    key = jax.random.key(seed)
    ks = jax.random.split(key, 16)

    def normal(k, shape, scale):
        return jax.random.normal(k, shape, F32) * scale

    L = DEPTH
    return {
        'x': normal(ks[0], (BATCH, SEQ, D_MODEL), 1.0),
        'norm_mix': 1.0 + normal(ks[1], (L, D_MODEL), 0.01),
        'w_in': normal(ks[2], (L, D_MODEL, IN_WIDTH), D_MODEL ** -0.5),
        'att_sinks': normal(ks[3], (L, ATT_HEADS), 0.5),
        'w_att_branch': normal(ks[4], (L, ATT_Q, D_MODEL), ATT_Q ** -0.5),
        'w_ret_branch': normal(ks[5], (L, RET_V, D_MODEL), RET_V ** -0.5),
        'w_out': normal(ks[6], (L, D_MODEL, D_MODEL), D_MODEL ** -0.5),
        'norm_ffn': 1.0 + normal(ks[7], (L, D_MODEL), 0.01),
        'peer_w_query': normal(ks[8], (L, D_MODEL, PEER_HEADS * PEER_QUERY_DIM), D_MODEL ** -0.5),
        'peer_sub_keys_1': normal(ks[9], (L, N_KEYS, PEER_HALF), PEER_HALF ** -0.5),
        'peer_sub_keys_2': normal(ks[10], (L, N_KEYS, PEER_HALF), PEER_HALF ** -0.5),
        'peer_u': normal(ks[11], (L, N_EXPERTS, D_MODEL), D_MODEL ** -0.5),
        'peer_v': normal(ks[12], (L, N_EXPERTS, D_MODEL), 0.5),
        'norm_final': 1.0 + normal(ks[13], (D_MODEL,), 0.01),
    }


def reference(x, norm_mix, w_in, att_sinks, w_att_branch, w_ret_branch, w_out, norm_ffn,
              peer_w_query, peer_sub_keys_1, peer_sub_keys_2, peer_u, peer_v, norm_final):
    positions = jnp.arange(x.shape[1], dtype=jnp.int32)
    h = x
    for layer in range(DEPTH):
        h = h + hybrid_mixer(rms_norm(h, norm_mix[layer]), positions, w_in[layer], att_sinks[layer],
                             w_att_branch[layer], w_ret_branch[layer], w_out[layer])
        h = h + peer_ffn(rms_norm(h, norm_ffn[layer]), peer_w_query[layer], peer_sub_keys_1[layer],
                         peer_sub_keys_2[layer], peer_u[layer], peer_v[layer])
    return rms_norm(h, norm_final)
```

```python
import functools

import jax
import jax.numpy as jnp
import numpy as np
from jax import lax
from jax.experimental import pallas as pl
from jax.experimental.pallas import tpu as pltpu

F32 = jnp.float32
BF16 = jnp.bfloat16

D_MODEL = 1024
ATT_HEADS = 8
ATT_KV_HEADS = 2
ATT_GROUP = ATT_HEADS // ATT_KV_HEADS
ATT_HEAD_DIM = 64
WINDOW = 128
ROPE_THETA = 500000.0
ROPE_DIMS = ATT_HEAD_DIM // 4
RET_HEADS = 8
RET_QK_DIM = 64
RET_V_DIM = 128
RET_CHUNK = 128
RET_THETA = 10000.0
ATT_Q = ATT_HEADS * ATT_HEAD_DIM
ATT_KV = ATT_KV_HEADS * ATT_HEAD_DIM
RET_QK = RET_HEADS * RET_QK_DIM
RET_V = RET_HEADS * RET_V_DIM
IN_SIZES = (ATT_Q, ATT_KV, ATT_KV, RET_QK, RET_QK, RET_V, RET_V, D_MODEL, D_MODEL)
IN_OFFSETS = tuple(int(v) for v in np.cumsum((0,) + IN_SIZES))
IN_WIDTH = IN_OFFSETS[-1]
PEER_HEADS = 8
N_KEYS = 128
N_EXPERTS = N_KEYS * N_KEYS
PEER_HALF = 128
PEER_TOPK = 16
EPS = 1e-6

LANES = 128
CHUNK = 128
NEG = -1e30
VMEM_LIMIT = 56 * 1024 * 1024

TM_MIX = 256
TB_ROUTE = 256
TB_PEER = 512
EB_PEER = 2048

NT_DIMS = (((1,), (1,)), ((), ()))
TN_DIMS = (((0,), (0,)), ((), ()))


def _rms(x, gain):
    return x * lax.rsqrt(jnp.mean(x * x, axis=-1, keepdims=True) + EPS) * gain


def _rope(x, cos, sin_lo, sin_hi, shift):
    outs = []
    for j in range(x.shape[1] // LANES):
        xb = x[:, j * LANES:(j + 1) * LANES]
        outs.append(xb * cos
                    + pltpu.roll(xb, LANES - shift, 1) * sin_lo
                    + pltpu.roll(xb, shift, 1) * sin_hi)
    return outs[0] if len(outs) == 1 else jnp.concatenate(outs, axis=1)


def _rope_tables(seq, rot_dims, theta):
    half = rot_dims // 2
    inv_freq = theta ** (-jnp.arange(half, dtype=F32) / half)
    ang = jnp.arange(seq, dtype=F32)[:, None] * inv_freq[None, :]
    cos, sin = jnp.cos(ang), jnp.sin(ang)
    pad = ATT_HEAD_DIM - rot_dims
    ones = jnp.ones((seq, pad), F32)
    zeros = jnp.zeros((seq, pad), F32)
    zh = jnp.zeros((seq, half), F32)
    c = jnp.concatenate([cos, cos, ones], axis=1)
    lo = jnp.concatenate([-sin, zh, zeros], axis=1)
    hi = jnp.concatenate([zh, sin, zeros], axis=1)
    rep = LANES // ATT_HEAD_DIM
    return tuple(jnp.tile(t, (1, rep)) for t in (c, lo, hi))


def _retention_tables():
    c = RET_CHUNK
    log_gamma = jnp.log1p(-jnp.exp2(-5.0 - jnp.arange(RET_HEADS, dtype=F32)))
    idx = jnp.arange(c, dtype=F32)
    diff = idx[:, None] - idx[None, :]
    decay = jnp.where(diff[None] >= 0,
                      jnp.exp(log_gamma[:, None, None] * jnp.maximum(diff, 0.0)[None]), 0.0)
    w_key = jnp.exp(log_gamma[:, None] * (c - 1.0 - idx)[None, :])
    w_query = jnp.exp(log_gamma[:, None] * (idx + 1.0)[None, :])
    chunk_decay = jnp.exp(log_gamma * c)
    bcast = lambda w: jnp.broadcast_to(w[:, :, None], (RET_HEADS, c, RET_QK_DIM))
    return decay, bcast(w_key), bcast(w_query), chunk_decay


def _mixer_kernel(sinks_ref, cdec_ref, x_ref, nmix_ref, win_ref, watt_ref, wret_ref, wout_ref,
                  nffn_ref, ca_ref, sal_ref, sah_ref, cr_ref, srl_ref, srh_ref,
                  decay_ref, wkey_ref, wqry_ref,
                  h1_ref, hn2_ref,
                  kprev_ref, vprev_ref, state_ref, att_ref, ret_ref):
    step = pl.program_id(1)
    tm = x_ref.shape[1]
    nchunk = tm // CHUNK

    @pl.when(step == 0)
    def _reset_carry():
        kprev_ref[...] = jnp.zeros_like(kprev_ref)
        vprev_ref[...] = jnp.zeros_like(vprev_ref)
        state_ref[...] = jnp.zeros_like(state_ref)

    x = x_ref[0]
    hn = _rms(x, nmix_ref[...]).astype(BF16)

    def proj(seg):
        return jnp.dot(hn, win_ref[:, IN_OFFSETS[seg]:IN_OFFSETS[seg + 1]],
                       preferred_element_type=F32)

    att_tabs = (ca_ref[...], sal_ref[...], sah_ref[...], ROPE_DIMS // 2)
    ret_tabs = (cr_ref[...], srl_ref[...], srh_ref[...], RET_QK_DIM // 2)
    qa = _rope(proj(0), *att_tabs).astype(BF16)
    ka = _rope(proj(1), *att_tabs).astype(BF16)
    va = proj(2).astype(BF16)
    qr = _rope(proj(3), *ret_tabs)
    kr = _rope(proj(4), *ret_tabs) * (RET_QK_DIM ** -0.5)
    vr = proj(5).astype(BF16)

    row = lax.broadcasted_iota(jnp.int32, (CHUNK, 2 * CHUNK), 0)
    col = lax.broadcasted_iota(jnp.int32, (CHUNK, 2 * CHUNK), 1)
    cur_ok = (col >= CHUNK) & (col - CHUNK <= row)

    for c in range(nchunk):
        r0 = c * CHUNK
        rows = slice(r0, r0 + CHUNK)
        if c == 0:
            kp, vp = kprev_ref[...], vprev_ref[...]
        else:
            kp, vp = ka[r0 - CHUNK:r0], va[r0 - CHUNK:r0]
        kcat = jnp.concatenate([kp, ka[rows]], axis=0)
        vcat = jnp.concatenate([vp, va[rows]], axis=0)
        first = jnp.where(step * nchunk + c == 0, CHUNK, 0)
        mask = cur_ok | ((col < CHUNK) & (col > row + first))
        for h in range(ATT_HEADS):
            g = h // ATT_GROUP
            hs = slice(h * ATT_HEAD_DIM, (h + 1) * ATT_HEAD_DIM)
            gs = slice(g * ATT_HEAD_DIM, (g + 1) * ATT_HEAD_DIM)
            s = lax.dot_general(qa[rows, hs], kcat[:, gs], NT_DIMS,
                                preferred_element_type=F32) * (ATT_HEAD_DIM ** -0.5)
            s = jnp.where(mask, s, NEG)
            sink = sinks_ref[h]
            m = jnp.maximum(jnp.max(s, axis=-1, keepdims=True), sink)
            p = jnp.exp(s - m)
            den = jnp.sum(p, axis=-1, keepdims=True) + jnp.exp(sink - m)
            o = jnp.dot(p.astype(BF16), vcat[:, gs], preferred_element_type=F32) / den
            att_ref[rows, hs] = o.astype(BF16)

        for h in range(RET_HEADS):
            hs = slice(h * RET_QK_DIM, (h + 1) * RET_QK_DIM)
            vs = slice(h * RET_V_DIM, (h + 1) * RET_V_DIM)
            q, k, v = qr[rows, hs], kr[rows, hs], vr[rows, vs]
            sc = lax.dot_general(q.astype(BF16), k.astype(BF16), NT_DIMS,
                                 preferred_element_type=F32) * decay_ref[h]
            st = state_ref[h]
            o = (jnp.dot(sc.astype(BF16), v, preferred_element_type=F32)
                 + jnp.dot((q * wqry_ref[h]).astype(BF16), st.astype(BF16),
                           preferred_element_type=F32))
            kv = lax.dot_general((k * wkey_ref[h]).astype(BF16), v, TN_DIMS,
                                 preferred_element_type=F32)
            state_ref[h] = st * cdec_ref[h] + kv
            ret_ref[rows, vs] = o * lax.rsqrt(jnp.mean(o * o, axis=-1, keepdims=True) + EPS)

    kprev_ref[...] = ka[tm - CHUNK:]
    vprev_ref[...] = va[tm - CHUNK:]

    gr = proj(6)
    ret = (gr * jax.nn.sigmoid(gr) * ret_ref[...]).astype(BF16)
    ya = jnp.dot(att_ref[...], watt_ref[...], preferred_element_type=F32)
    yr = jnp.dot(ret, wret_ref[...], preferred_element_type=F32)
    merged = jax.nn.sigmoid(proj(7)) * ya + jax.nn.sigmoid(proj(8)) * yr
    h1 = x + jnp.dot(merged.astype(BF16), wout_ref[...], preferred_element_type=F32)
    h1_ref[0] = h1
    hn2_ref[0] = _rms(h1, nffn_ref[...]).astype(BF16)


def _const_spec(shape):
    nd = len(shape)
    return pl.BlockSpec(shape, lambda *_: (0,) * nd, pipeline_mode=pl.Buffered(1))


def _mixer(x, norm_mix, w_in, sinks, w_att, w_ret, w_out, norm_ffn):
    b, s, d = x.shape
    tm = TM_MIX
    ca, sal, sah = _rope_tables(s, ROPE_DIMS, ROPE_THETA)
    cr, srl, srh = _rope_tables(s, RET_QK_DIM, RET_THETA)
    decay, wkey, wqry, cdec = _retention_tables()
    smem = pl.BlockSpec(memory_space=pltpu.SMEM)
    tab = pl.BlockSpec((tm, LANES), lambda bi, si: (si, 0))
    tok = pl.BlockSpec((1, tm, d), lambda bi, si: (bi, si, 0))
    return pl.pallas_call(
        _mixer_kernel,
        grid=(b, s // tm),
        in_specs=[smem, smem, tok, _const_spec((1, d)), _const_spec((d, IN_WIDTH)),
                  _const_spec((ATT_Q, d)), _const_spec((RET_V, d)), _const_spec((d, d)),
                  _const_spec((1, d)), tab, tab, tab, tab, tab, tab,
                  _const_spec((RET_HEADS, CHUNK, CHUNK)),
                  _const_spec((RET_HEADS, CHUNK, RET_QK_DIM)),
                  _const_spec((RET_HEADS, CHUNK, RET_QK_DIM))],
        out_specs=[tok, tok],
        out_shape=[jax.ShapeDtypeStruct((b, s, d), F32), jax.ShapeDtypeStruct((b, s, d), BF16)],
        scratch_shapes=[pltpu.VMEM((CHUNK, ATT_KV), BF16), pltpu.VMEM((CHUNK, ATT_KV), BF16),
                        pltpu.VMEM((RET_HEADS, RET_QK_DIM, RET_V_DIM), F32),
                        pltpu.VMEM((tm, ATT_Q), BF16), pltpu.VMEM((tm, RET_V), F32)],
        compiler_params=pltpu.CompilerParams(
            dimension_semantics=("arbitrary", "arbitrary"), vmem_limit_bytes=VMEM_LIMIT),
        name="mixer",
    )(sinks, cdec, x, norm_mix.reshape(1, d), w_in.astype(BF16), w_att.astype(BF16),
      w_ret.astype(BF16), w_out.astype(BF16), norm_ffn.reshape(1, d),
      ca, sal, sah, cr, srl, srh, decay, wkey, wqry)


def _top16(s):
    work = s
    rank = jnp.full(s.shape, float(PEER_TOPK), F32)
    vals = []
    for r in range(PEER_TOPK):
        m = jnp.max(work, axis=0, keepdims=True)
        hit = work == m
        rank = jnp.where(hit, float(r), rank)
        work = jnp.where(hit, NEG, work)
        vals.append(m)
    return jnp.concatenate(vals, axis=0), rank


def _route_kernel(hn_ref, wq_ref, k1_ref, k2_ref, w2_ref, rk2_ref, w1_ref, cnt_ref):
    tb = hn_ref.shape[0]
    q = jnp.dot(hn_ref[...], wq_ref[...], preferred_element_type=F32).astype(BF16)
    k1, k2 = k1_ref[...], k2_ref[...]
    for t in range(tb // LANES):
        rows = slice(t * LANES, (t + 1) * LANES)
        for h in range(PEER_HEADS):
            c0 = h * 2 * PEER_HALF
            s1 = lax.dot_general(k1, q[rows, c0:c0 + PEER_HALF], NT_DIMS,
                                 preferred_element_type=F32)
            s2 = lax.dot_general(k2, q[rows, c0 + PEER_HALF:c0 + 2 * PEER_HALF], NT_DIMS,
                                 preferred_element_type=F32)
            v1, rank1 = _top16(s1)
            v2, rank2 = _top16(s2)
            slabs = [v1[r:r + 1] + v2 for r in range(PEER_TOPK)]
            work = jnp.concatenate(slabs, axis=0)
            for _ in range(PEER_TOPK):
                tau = jnp.max(work, axis=0, keepdims=True)
                work = jnp.where(work == tau, NEG, work)
            top = v1[0:1] + v2[0:1]
            z = jnp.zeros_like(top)
            cnt = jnp.zeros(s1.shape, F32)
            for r in range(PEER_TOPK):
                sel = slabs[r] >= tau
                z = z + jnp.sum(jnp.where(sel, jnp.exp(slabs[r] - top), 0.0), axis=0, keepdims=True)
                n_r = jnp.sum(jnp.where(sel, 1.0, 0.0), axis=0, keepdims=True)
                cnt = jnp.where(rank1 == float(r), n_r, cnt)
            w1_ref[h, :, rows] = jnp.exp(s1 - v1[0:1]) / z
            cnt_ref[h, :, rows] = cnt
            w2_ref[h, :, rows] = jnp.exp(s2 - v2[0:1]).astype(BF16)
            rk2_ref[h, :, rows] = rank2.astype(BF16)


def _route(hn2, w_query, keys1, keys2):
    t, d = hn2.shape
    tb = TB_ROUTE
    wq_cols = w_query.shape[1]
    tab = pl.BlockSpec((PEER_HEADS, N_KEYS, tb), lambda i: (0, 0, i))
    shape = (PEER_HEADS, N_KEYS, t)
    return pl.pallas_call(
        _route_kernel,
        grid=(t // tb,),
        in_specs=[pl.BlockSpec((tb, d), lambda i: (i, 0)), _const_spec((d, wq_cols)),
                  _const_spec((N_KEYS, PEER_HALF)), _const_spec((N_KEYS, PEER_HALF))],
        out_specs=[tab, tab, tab, tab],
        out_shape=[jax.ShapeDtypeStruct(shape, BF16), jax.ShapeDtypeStruct(shape, BF16),
                   jax.ShapeDtypeStruct(shape, F32), jax.ShapeDtypeStruct(shape, F32)],
        compiler_params=pltpu.CompilerParams(
            dimension_semantics=("arbitrary",), vmem_limit_bytes=VMEM_LIMIT),
        name="peer_route",
    )(hn2, w_query.astype(BF16), keys1.astype(BF16), keys2.astype(BF16))


def _gelu(a):
    return 0.5 * a * (1.0 + lax.erf(a * float(np.sqrt(0.5))))


def _expert_kernel(hn_ref, u_ref, vt_ref, w2_ref, rk2_ref, w1_ref, cnt_ref, h1_ref, nfin_ref,
                   out_ref, acc_ref, coef_ref):
    j = pl.program_id(1)
    eb = u_ref.shape[0]
    keys_per_tile = eb // N_KEYS

    @pl.when(j == 0)
    def _zero_acc():
        acc_ref[...] = jnp.zeros_like(acc_ref)

    act = _gelu(lax.dot_general(u_ref[...], hn_ref[...], NT_DIMS, preferred_element_type=F32))
    for al in range(keys_per_tile):
        a = j * keys_per_tile + al
        gate = None
        for h in range(PEER_HEADS):
            n_sel = cnt_ref[h, pl.ds(a, 1), :].astype(BF16)
            w_a = w1_ref[h, pl.ds(a, 1), :].astype(BF16)
            term = jnp.where(rk2_ref[h] < n_sel, w2_ref[h], jnp.zeros((), BF16)) * w_a
            gate = term if gate is None else gate + term
        rows = slice(al * N_KEYS, (al + 1) * N_KEYS)
        coef_ref[rows, :] = act[rows].astype(BF16) * gate
    acc_ref[...] += jnp.dot(vt_ref[...], coef_ref[...], preferred_element_type=F32)

    @pl.when(j == pl.num_programs(1) - 1)
    def _finish():
        h2 = h1_ref[...] + acc_ref[...].T
        out_ref[...] = _rms(h2, nfin_ref[...])


def _experts(hn2, u_bf, vt_bf, w2, rk2, w1, cnt, h1, norm_final):
    t, d = hn2.shape
    tb, eb = TB_PEER, EB_PEER
    tab = pl.BlockSpec((PEER_HEADS, N_KEYS, tb), lambda i, j: (0, 0, i))
    tok = pl.BlockSpec((tb, d), lambda i, j: (i, 0))
    return pl.pallas_call(
        _expert_kernel,
        grid=(t // tb, N_EXPERTS // eb),
        in_specs=[tok, pl.BlockSpec((eb, d), lambda i, j: (j, 0)),
                  pl.BlockSpec((d, eb), lambda i, j: (0, j)),
                  tab, tab, tab, tab, tok, _const_spec((1, d))],
        out_specs=tok,
        out_shape=jax.ShapeDtypeStruct((t, d), F32),
        scratch_shapes=[pltpu.VMEM((d, tb), F32), pltpu.VMEM((eb, tb), BF16)],
        compiler_params=pltpu.CompilerParams(
            dimension_semantics=("arbitrary", "arbitrary"), vmem_limit_bytes=VMEM_LIMIT),
        name="peer_experts",
    )(hn2, u_bf, vt_bf, w2, rk2, w1, cnt, h1, norm_final.reshape(1, d))


def kernel(x, norm_mix, w_in, att_sinks, w_att_branch, w_ret_branch, w_out, norm_ffn, peer_w_query, peer_sub_keys_1, peer_sub_keys_2, peer_u, peer_v, norm_final):
    b, s, d = x.shape
    assert norm_mix.shape[0] == 1, "single-layer block"
    h1, hn2 = _mixer(x, norm_mix[0], w_in[0], att_sinks[0], w_att_branch[0], w_ret_branch[0],
                     w_out[0], norm_ffn[0])
    h1 = h1.reshape(b * s, d)
    hn2 = hn2.reshape(b * s, d)
    w2, rk2, w1, cnt = _route(hn2, peer_w_query[0], peer_sub_keys_1[0], peer_sub_keys_2[0])
    out = _experts(hn2, peer_u[0].astype(BF16), peer_v[0].astype(BF16).T, w2, rk2, w1, cnt,
                   h1, norm_final)
    return out.reshape(b, s, d)
```
